```python
import math
import jax
import jax.numpy as jnp
from jax import lax
import numpy as np

D_MODEL = 1024
BATCH = 8
SEQ = 2048
DEPTH = 1
DEC_BATCH = 16
DEC_SEQ = 4096
PAST_LEN = 128

RMS_EPS = 1e-6
ATT_HEADS = 4
ATT_QK_DIM = 64
ATT_V_DIM = 2 * ATT_QK_DIM
ATT_WIDTH = ATT_HEADS * ATT_V_DIM
ROPE_DIM = ATT_QK_DIM // 4
ROPE_THETA = 500000.0
Q_BLOCK = 128
SSM_HEADS = 8
SSM_HEAD_DIM = 64
SSM_WIDTH = SSM_HEADS * SSM_HEAD_DIM
SSM_GROUPS = 2
SSM_STATE = 128
CONV_WIDTH = 5
CHUNK = 128
DT_MIN = 0.001
DT_MAX = 0.1
MIX_WIDTH = ATT_WIDTH + SSM_WIDTH
Q_COLS = ATT_HEADS * 2 * ATT_QK_DIM
K_COLS = ATT_HEADS * 2 * ATT_QK_DIM
V_COLS = ATT_WIDTH
Z_COLS = SSM_WIDTH
XBC_COLS = SSM_WIDTH + 2 * SSM_GROUPS * SSM_STATE
DT_COLS = 2 * SSM_HEADS
IN_COLS = Q_COLS + K_COLS + V_COLS + Z_COLS + XBC_COLS + DT_COLS
IN_SPLITS = (Q_COLS, Q_COLS + K_COLS, Q_COLS + K_COLS + V_COLS,
             Q_COLS + K_COLS + V_COLS + Z_COLS,
             Q_COLS + K_COLS + V_COLS + Z_COLS + XBC_COLS)
N_EXPERTS = 16
EXPERT_FF = 1024
CAPACITY_FACTOR = 2

kernel_name = 'hybrid_diffattn_ssd_expert_choice_encoder'


def rmsnorm(x, g):
    xf = x.astype(jnp.float32)
    y = xf * lax.rsqrt(jnp.mean(xf * xf, axis=-1, keepdims=True) + RMS_EPS)
    return (y * g.astype(jnp.float32)).astype(x.dtype)


def apply_partial_rope(t, positions):
    half = ROPE_DIM // 2
    inv_freq = jnp.power(ROPE_THETA, -jnp.arange(half, dtype=jnp.float32) * 2.0 / ROPE_DIM)
    ang = positions.astype(jnp.float32)[:, None] * inv_freq[None, :]
    cos = jnp.cos(ang)[None, :, None, None, :].astype(t.dtype)
    sin = jnp.sin(ang)[None, :, None, None, :].astype(t.dtype)
    t1 = t[..., :half]
    t2 = t[..., half:ROPE_DIM]
    rest = t[..., ROPE_DIM:]
    return jnp.concatenate([t1 * cos - t2 * sin, t2 * cos + t1 * sin, rest], axis=-1)


def differential_attention(q, k, v, lam, sub_g, lam_init):
    b, s = q.shape[0], q.shape[1]
    nb = s // Q_BLOCK
    qs = q * jnp.asarray(ATT_QK_DIM ** -0.5, q.dtype)
    qb = qs.reshape(b, nb, Q_BLOCK, ATT_HEADS, 2, ATT_QK_DIM).transpose(1, 0, 2, 3, 4, 5)

    def block(q_blk):
        sc = jnp.einsum('bqhcd,bkhcd->bhcqk', q_blk, k, preferred_element_type=jnp.float32)
        p = jax.nn.softmax(sc, axis=-1)
        w = p[:, :, 0] - lam * p[:, :, 1]
        return jnp.einsum('bhqk,bkhe->bqhe', w.astype(v.dtype), v)

    o = lax.map(block, qb)
    o = o.transpose(1, 0, 2, 3, 4).reshape(b, s, ATT_HEADS, ATT_V_DIM)
    o = rmsnorm(o, sub_g) * jnp.asarray(1.0 - lam_init, o.dtype)
    return o.reshape(b, s, ATT_WIDTH)


def centred_depthwise_conv(u, w, bias):
    pad = CONV_WIDTH // 2
    out = lax.conv_general_dilated(
        u, w[:, None, :].astype(u.dtype), window_strides=(1,), padding=[(pad, pad)],
        dimension_numbers=('NWC', 'WIO', 'NWC'), feature_group_count=u.shape[-1])
    return out + bias.astype(u.dtype)


def segsum_exp(a):
    cs = jnp.cumsum(a, axis=-1)
    diff = cs[..., :, None] - cs[..., None, :]
    n = a.shape[-1]
    mask = jnp.tril(jnp.ones((n, n), dtype=bool))
    return jnp.exp(jnp.where(mask, diff, -jnp.inf))


def ssd_chunked_scan(x, dt, a_neg, bm, cm):
    b, l = x.shape[0], x.shape[1]
    nc = l // CHUNK
    r = SSM_HEADS // SSM_GROUPS
    xf = (x.astype(jnp.float32) * dt[..., None]).reshape(b, nc, CHUNK, SSM_GROUPS, r, SSM_HEAD_DIM)
    a = (dt * a_neg).reshape(b, nc, CHUNK, SSM_GROUPS, r).transpose(0, 1, 3, 4, 2)
    bc = bm.astype(jnp.float32).reshape(b, nc, CHUNK, SSM_GROUPS, SSM_STATE)
    cc = cm.astype(jnp.float32).reshape(b, nc, CHUNK, SSM_GROUPS, SSM_STATE)
    a_cs = jnp.cumsum(a, axis=-1)
    lmat = segsum_exp(a)
    cb = jnp.einsum('bclgn,bcsgn->bcgls', cc, bc)
    y_diag = jnp.einsum('bcgls,bcgrls,bcsgrp->bclgrp', cb, lmat, xf)
    decay_states = jnp.exp(a_cs[..., -1:] - a_cs)
    states = jnp.einsum('bclgn,bcgrl,bclgrp->bcgrpn', bc, decay_states, xf)
    chunk_decay = jnp.exp(a_cs[..., -1])

    def step(h, inp):
        st, dec = inp
        return h * dec[..., None, None] + st, h

    h0 = jnp.zeros((b, SSM_GROUPS, r, SSM_HEAD_DIM, SSM_STATE), jnp.float32)
    _, prev = lax.scan(step, h0, (states.transpose(1, 0, 2, 3, 4, 5), chunk_decay.transpose(1, 0, 2, 3)))
    prev = prev.transpose(1, 0, 2, 3, 4, 5)
    y_off = jnp.einsum('bclgn,bcgrpn,bcgrl->bclgrp', cc, prev, jnp.exp(a_cs))
    return (y_diag + y_off).reshape(b, l, SSM_HEADS, SSM_HEAD_DIM)


def flip_seq(t):
    return jnp.flip(t, axis=1)


def bidirectional_ssd(z, xbc, dt_raw, conv_w, conv_b, dt_bias, a_log, d_skip, norm_g):
    b, s = z.shape[0], z.shape[1]
    xbc = jax.nn.silu(centred_depthwise_conv(xbc, conv_w, conv_b))
    xs, bm, cm = jnp.split(xbc, (SSM_WIDTH, SSM_WIDTH + SSM_GROUPS * SSM_STATE), axis=-1)
    xs = xs.reshape(b, s, SSM_HEADS, SSM_HEAD_DIM)
    bm = bm.reshape(b, s, SSM_GROUPS, SSM_STATE)
    cm = cm.reshape(b, s, SSM_GROUPS, SSM_STATE)
    y = xs.astype(jnp.float32) * d_skip.astype(jnp.float32)[:, None]
    for direction in range(2):
        dt = jax.nn.softplus((dt_raw[:, :, direction] + dt_bias[direction]).astype(jnp.float32))
        a_neg = -jnp.exp(a_log[direction].astype(jnp.float32))
        if direction == 0:
            y = y + ssd_chunked_scan(xs, dt, a_neg, bm, cm)
        else:
            y = y + flip_seq(ssd_chunked_scan(flip_seq(xs), flip_seq(dt), a_neg, flip_seq(bm), flip_seq(cm)))
    y = y.reshape(b, s, SSM_WIDTH).astype(z.dtype) * jax.nn.silu(z)
    gs = SSM_WIDTH // SSM_GROUPS
    y = rmsnorm(y.reshape(b, s, SSM_GROUPS, gs), norm_g.reshape(SSM_GROUPS, gs))
    return y.reshape(b, s, SSM_WIDTH)


def token_mixers(h, w_in, lam_qk, attn_sub_g, conv_w, conv_b, dt_bias, a_log, d_skip, ssm_norm_g, w_out, lam_init):
    b, s = h.shape[0], h.shape[1]
    proj = jnp.einsum('bsd,dc->bsc', h, w_in)
    q, k, v, z, xbc, dt_raw = jnp.split(proj, IN_SPLITS, axis=-1)
    pos = jnp.arange(s)
    q = apply_partial_rope(q.reshape(b, s, ATT_HEADS, 2, ATT_QK_DIM), pos)
    k = apply_partial_rope(k.reshape(b, s, ATT_HEADS, 2, ATT_QK_DIM), pos)
    v = v.reshape(b, s, ATT_HEADS, ATT_V_DIM)
    lq = lam_qk.astype(jnp.float32)
    lam = jnp.exp(jnp.sum(lq[0] * lq[1])) - jnp.exp(jnp.sum(lq[2] * lq[3])) + lam_init
    att = differential_attention(q, k, v, lam, attn_sub_g, lam_init)
    ssm = bidirectional_ssd(z, xbc, dt_raw.reshape(b, s, 2, SSM_HEADS), conv_w, conv_b,
                            dt_bias, a_log, d_skip, ssm_norm_g)
    return jnp.einsum('bsc,cd->bsd', jnp.concatenate([att, ssm], axis=-1), w_out)


def expert_choice_ffn(x, w_router, w_gate, w_up, w_down):
    b, s, d = x.shape
    t = b * s
    cap = CAPACITY_FACTOR * t // N_EXPERTS
    xf = x.reshape(t, d)
    logits = jnp.einsum('td,de->te', xf, w_router, preferred_element_type=jnp.float32)
    aff = jax.nn.softmax(logits, axis=-1)
    gate, idx = lax.top_k(aff.T, cap)
    xe = xf[idx]
    hid = jax.nn.silu(jnp.einsum('ecd,edf->ecf', xe, w_gate)) * jnp.einsum('ecd,edf->ecf', xe, w_up)
    ye = jnp.einsum('ecf,efd->ecd', hid, w_down) * gate[..., None].astype(x.dtype)
    out = jnp.zeros_like(xf).at[idx.reshape(-1)].add(ye.reshape(-1, d))
    return out.reshape(b, s, d)


def setup_inputs(seed: int = 0) -> dict:
    key = jax.random.key(seed)
    ks = jax.random.split(key, 20)
    f = jnp.float32
    dt0 = jnp.exp(jax.random.uniform(ks[8], (DEPTH, 2, SSM_HEADS), f)
                  * (math.log(DT_MAX) - math.log(DT_MIN)) + math.log(DT_MIN))
    dt_bias = dt0 + jnp.log(-jnp.expm1(-dt0))
    a_log = jnp.log(jax.random.uniform(ks[9], (DEPTH, 2, SSM_HEADS), f, 1.0, 16.0))
    return {
        'x_prompt': jax.random.normal(ks[0], (BATCH, SEQ, D_MODEL), f),
        'x_sample': jax.random.normal(ks[1], (DEC_BATCH, DEC_SEQ, D_MODEL), f),
        'norm1_g': 1.0 + 0.01 * jax.random.normal(ks[2], (DEPTH, D_MODEL), f),
        'w_in': jax.random.normal(ks[3], (DEPTH, D_MODEL, IN_COLS), f) * D_MODEL ** -0.5,
        'lam_qk': 0.1 * jax.random.normal(ks[4], (DEPTH, 4, ATT_QK_DIM), f),
        'attn_sub_g': 1.0 + 0.01 * jax.random.normal(ks[5], (DEPTH, ATT_V_DIM), f),
        'conv_w': jax.random.normal(ks[6], (DEPTH, CONV_WIDTH, XBC_COLS), f) * CONV_WIDTH ** -0.5,
        'conv_b': 0.01 * jax.random.normal(ks[7], (DEPTH, XBC_COLS), f),
        'dt_bias': dt_bias,
        'a_log': a_log,
        'd_skip': 1.0 + 0.01 * jax.random.normal(ks[10], (DEPTH, SSM_HEADS), f),
        'ssm_norm_g': 1.0 + 0.01 * jax.random.normal(ks[11], (DEPTH, SSM_WIDTH), f),
        'w_out': jax.random.normal(ks[12], (DEPTH, MIX_WIDTH, D_MODEL), f) * MIX_WIDTH ** -0.5,
        'norm2_g': 1.0 + 0.01 * jax.random.normal(ks[13], (DEPTH, D_MODEL), f),
        'w_router': jax.random.normal(ks[14], (DEPTH, D_MODEL, N_EXPERTS), f) * D_MODEL ** -0.5,
        'w_gate': jax.random.normal(ks[15], (DEPTH, N_EXPERTS, D_MODEL, EXPERT_FF), f) * D_MODEL ** -0.5,
        'w_up': jax.random.normal(ks[16], (DEPTH, N_EXPERTS, D_MODEL, EXPERT_FF), f) * D_MODEL ** -0.5,
        'w_down': jax.random.normal(ks[17], (DEPTH, N_EXPERTS, EXPERT_FF, D_MODEL), f) * EXPERT_FF ** -0.5,
        'final_g': 1.0 + 0.01 * jax.random.normal(ks[18], (D_MODEL,), f),
    }


def reference(x_prompt, x_sample, norm1_g, w_in, lam_qk, attn_sub_g, conv_w, conv_b, dt_bias, a_log,
              d_skip, ssm_norm_g, w_out, norm2_g, w_router, w_gate, w_up, w_down, final_g):
    def trunk(x):
        for l in range(DEPTH):
            lam_init = 0.8 - 0.6 * math.exp(-0.3 * l)
            h = rmsnorm(x, norm1_g[l])
            x = x + token_mixers(h, w_in[l], lam_qk[l], attn_sub_g[l], conv_w[l], conv_b[l], dt_bias[l],
                                 a_log[l], d_skip[l], ssm_norm_g[l], w_out[l], lam_init)
            h = rmsnorm(x, norm2_g[l])
            x = x + expert_choice_ffn(h, w_router[l], w_gate[l], w_up[l], w_down[l])
        return rmsnorm(x, final_g)

    y_prompt = trunk(x_prompt)
    y_sample = trunk(x_sample)
    return (y_prompt, y_sample)
```

```python
import functools
import math

import jax
import jax.numpy as jnp
from jax import lax
from jax.experimental import pallas as pl
from jax.experimental.pallas import tpu as pltpu

F32 = jnp.float32
BF16 = jnp.bfloat16
HIGHEST = lax.Precision.HIGHEST

D_MODEL = 1024
RMS_EPS = 1e-6
ATT_HEADS = 4
ATT_QK_DIM = 64
ATT_V_DIM = 128
ATT_WIDTH = 512
ROPE_DIM = 16
ROPE_THETA = 500000.0
SSM_HEADS = 8
SSM_HEAD_DIM = 64
SSM_WIDTH = 512
SSM_GROUPS = 2
SSM_STATE = 128
CONV_WIDTH = 5
CHUNK = 128
XBC_COLS = 1024
N_EXPERTS = 16
EXPERT_FF = 1024
CAPACITY_FACTOR = 2

LANES = 128
ROW_ALIGN = 16
IN_PAD = 3328
AUG = 128
XE_COLS = D_MODEL + AUG

TM = 512
TQ = 256
TC = 512
TB = 256
NP_FAST = 64
NP_SLOW = TB + ROW_ALIGN
SLOW_GROUP = 4
SLOT_PAD = 512
BISECT_MAX_STEPS = 200
VMEM_LIMIT = 56 * 1024 * 1024


def _cparams(sem):
    return pltpu.CompilerParams(dimension_semantics=sem, vmem_limit_bytes=VMEM_LIMIT)


def _sigmoid(x):
    return 1.0 / (1.0 + jnp.exp(-x))


def _inproj_kernel(x_ref, g_ref, w_ref, cos_ref, sa_ref, sb_ref,
                   q_ref, k_ref, v_ref, z_ref, xbc_ref, dt_ref):
    x = x_ref[...]
    ms = jnp.mean(x * x, axis=-1, keepdims=True)
    h = (x * lax.rsqrt(ms + RMS_EPS) * g_ref[...]).astype(BF16)

    def proj(lo, hi):
        return jnp.dot(h, w_ref[:, lo:hi], preferred_element_type=F32)

    cos = cos_ref[...]
    sa = sa_ref[...]
    sb = sb_ref[...]
    for ref, off, scale in ((q_ref, 0, ATT_QK_DIM ** -0.5), (k_ref, 512, 1.0)):
        for c in range(4):
            t = proj(off + c * LANES, off + (c + 1) * LANES)
            r = t * cos + pltpu.roll(t, 8, 1) * sa + pltpu.roll(t, LANES - 8, 1) * sb
            ref[:, c * LANES:(c + 1) * LANES] = (r * scale).astype(BF16)
    v_ref[...] = proj(1024, 1536).astype(BF16)
    z_ref[...] = proj(1536, 2048)
    xbc_ref[...] = proj(2048, 3072)
    dt_ref[0] = proj(3072, 3200)
    dt_ref[1] = proj(3200, 3328)


def _inproj(xf, g, w, cos_t, sa_t, sb_t, seq):
    t = xf.shape[0]
    nps = seq // TM
    row = lambda i: (i, 0)
    pos = lambda i: (i % nps, 0)
    const = lambda i: (0, 0)
    return pl.pallas_call(
        _inproj_kernel,
        grid=(t // TM,),
        in_specs=[pl.BlockSpec((TM, D_MODEL), row),
                  pl.BlockSpec((1, D_MODEL), const),
                  pl.BlockSpec((D_MODEL, IN_PAD), const),
                  pl.BlockSpec((TM, LANES), pos),
                  pl.BlockSpec((TM, LANES), pos),
                  pl.BlockSpec((TM, LANES), pos)],
        out_specs=[pl.BlockSpec((TM, 512), row),
                   pl.BlockSpec((TM, 512), row),
                   pl.BlockSpec((TM, 512), row),
                   pl.BlockSpec((TM, 512), row),
                   pl.BlockSpec((TM, XBC_COLS), row),
                   pl.BlockSpec((2, TM, LANES), lambda i: (0, i, 0))],
        out_shape=[jax.ShapeDtypeStruct((t, 512), BF16),
                   jax.ShapeDtypeStruct((t, 512), BF16),
                   jax.ShapeDtypeStruct((t, 512), BF16),
                   jax.ShapeDtypeStruct((t, 512), F32),
                   jax.ShapeDtypeStruct((t, XBC_COLS), F32),
                   jax.ShapeDtypeStruct((2, t, LANES), F32)],
        compiler_params=_cparams(("parallel",)),
        name="inproj",
    )(xf, g, w, cos_t, sa_t, sb_t)


def _conv_kernel(x_ref, p_ref, n_ref, w_ref, b_ref, xs_ref, bc_ref, buf_ref):
    i = pl.program_id(1)
    last = pl.num_programs(1) - 1
    buf_ref[0:8, :] = jnp.where(i > 0, p_ref[...], 0.0)
    buf_ref[8:8 + TC, :] = x_ref[...]
    buf_ref[8 + TC:16 + TC, :] = jnp.where(i < last, n_ref[...], 0.0)
    acc = jnp.broadcast_to(b_ref[...], (TC, XBC_COLS))
    for j in range(CONV_WIDTH):
        acc = acc + w_ref[j:j + 1, :] * buf_ref[pl.ds(8 - CONV_WIDTH // 2 + j, TC), :]
    out = acc * _sigmoid(acc)
    xs_ref[...] = out[:, :SSM_WIDTH]
    bc_ref[...] = out[:, SSM_WIDTH:].astype(BF16)


def _conv(xbc, w, b):
    bsz, seq, _ = xbc.shape
    nt = seq // TC
    hb = TC // 8
    return pl.pallas_call(
        _conv_kernel,
        grid=(bsz, nt),
        in_specs=[pl.BlockSpec((None, TC, XBC_COLS), lambda bb, i: (bb, i, 0)),
                  pl.BlockSpec((None, 8, XBC_COLS), lambda bb, i: (bb, jnp.maximum(i * hb - 1, 0), 0)),
                  pl.BlockSpec((None, 8, XBC_COLS), lambda bb, i: (bb, jnp.minimum((i + 1) * hb, seq // 8 - 1), 0)),
                  pl.BlockSpec((CONV_WIDTH, XBC_COLS), lambda bb, i: (0, 0)),
                  pl.BlockSpec((1, XBC_COLS), lambda bb, i: (0, 0))],
        out_specs=[pl.BlockSpec((None, TC, SSM_WIDTH), lambda bb, i: (bb, i, 0)),
                   pl.BlockSpec((None, TC, 512), lambda bb, i: (bb, i, 0))],
        out_shape=[jax.ShapeDtypeStruct((bsz, seq, SSM_WIDTH), F32),
                   jax.ShapeDtypeStruct((bsz, seq, 512), BF16)],
        scratch_shapes=[pltpu.VMEM((TC + 16, XBC_COLS), F32)],
        compiler_params=_cparams(("parallel", "parallel")),
        name="conv",
    )(xbc, xbc, xbc, w, b)


def _ssd_kernel(xs_ref, bc_ref, dt_ref, dtb_ref, alog_ref, y_ref, st_ref):
    d = pl.program_id(0)
    c = pl.program_id(2)

    @pl.when(c == 0)
    def _():
        st_ref[...] = jnp.zeros_like(st_ref)

    L = CHUNK
    row = lax.broadcasted_iota(jnp.int32, (L, L), 0)
    col = lax.broadcasted_iota(jnp.int32, (L, L), 1)
    mask = (row - col) * (1 - 2 * d) >= 0
    tri = mask.astype(F32)

    raw = dt_ref[...] + dtb_ref[...]
    dt = jnp.maximum(raw, 0.0) + jnp.log(1.0 + jnp.exp(-jnp.abs(raw)))
    a = dt * (-jnp.exp(alog_ref[...]))
    cs_col = jnp.dot(tri, a, preferred_element_type=F32, precision=HIGHEST)
    cs_row = lax.dot_general(a, tri, (((0,), (1,)), ((), ())),
                             preferred_element_type=F32, precision=HIGHEST)
    tot = jnp.sum(a, axis=0, keepdims=True)
    e_cs = jnp.exp(cs_col)
    e_dec = jnp.exp(tot - cs_col)
    e_tot = jnp.exp(tot)

    xs = xs_ref[...]
    bc = bc_ref[...]
    ys = []
    for g in range(SSM_GROUPS):
        bg = bc[:, g * SSM_STATE:(g + 1) * SSM_STATE]
        cg = bc[:, 256 + g * SSM_STATE:256 + (g + 1) * SSM_STATE]
        cb = lax.dot_general(cg, bg, (((1,), (1,)), ((), ())), preferred_element_type=F32)
        for r in range(SSM_HEADS // SSM_GROUPS):
            h = g * (SSM_HEADS // SSM_GROUPS) + r
            csc = cs_col[:, h:h + 1]
            csr = cs_row[h:h + 1, :]
            lm = jnp.exp(jnp.where(mask, csc - csr, -jnp.inf))
            mh = (cb * lm).astype(BF16)
            xdt = xs[:, h * SSM_HEAD_DIM:(h + 1) * SSM_HEAD_DIM] * dt[:, h:h + 1]
            y_diag = jnp.dot(mh, xdt.astype(BF16), preferred_element_type=F32)
            s_prev = st_ref[h]
            y_off = jnp.dot(cg, s_prev.astype(BF16), preferred_element_type=F32) * e_cs[:, h:h + 1]
            xdw = (xdt * e_dec[:, h:h + 1]).astype(BF16)
            s_new = lax.dot_general(bg, xdw, (((0,), (0,)), ((), ())), preferred_element_type=F32)
            st_ref[h] = s_prev * e_tot[:, h:h + 1] + s_new
            ys.append(y_diag + y_off)
    y_ref[...] = jnp.concatenate(ys, axis=1)


def _ssd(xs, bc, dt2, dtb, alog):
    bsz, seq, _ = xs.shape
    nc = seq // CHUNK
    chunk = lambda d, bb, c: (bb, c + d * (nc - 1 - 2 * c), 0)
    return pl.pallas_call(
        _ssd_kernel,
        grid=(2, bsz, nc),
        in_specs=[pl.BlockSpec((None, CHUNK, SSM_WIDTH), chunk),
                  pl.BlockSpec((None, CHUNK, 512), chunk),
                  pl.BlockSpec((None, None, CHUNK, LANES), lambda d, bb, c: (d, bb, c + d * (nc - 1 - 2 * c), 0)),
                  pl.BlockSpec((None, 1, LANES), lambda d, bb, c: (d, 0, 0)),
                  pl.BlockSpec((None, 1, LANES), lambda d, bb, c: (d, 0, 0))],
        out_specs=pl.BlockSpec((None, None, CHUNK, SSM_WIDTH),
                               lambda d, bb, c: (d, bb, c + d * (nc - 1 - 2 * c), 0)),
        out_shape=jax.ShapeDtypeStruct((2, bsz, seq, SSM_WIDTH), F32),
        scratch_shapes=[pltpu.VMEM((SSM_HEADS, SSM_STATE, SSM_HEAD_DIM), F32)],
        compiler_params=_cparams(("arbitrary", "arbitrary", "arbitrary")),
        name="ssd",
    )(xs, bc, dt2, dtb, alog)


def _attn_kernel(q_ref, k_ref, v_ref, lq_ref, sg_ref, o_ref, *, lam_init):
    lq = lq_ref[...]
    lam = (jnp.exp(jnp.sum(lq[0:1] * lq[1:2], axis=1, keepdims=True))
           - jnp.exp(jnp.sum(lq[2:3] * lq[3:4], axis=1, keepdims=True)) + lam_init)
    q = q_ref[...]
    k = k_ref[...]
    v = v_ref[...]
    lane = lax.broadcasted_iota(jnp.int32, q.shape, 1)
    zero = jnp.zeros_like(q)
    ws = []
    for comp in range(2):
        keep = (lane < ATT_QK_DIM) if comp == 0 else (lane >= ATT_QK_DIM)
        qc = jnp.where(keep, q, zero)
        s = lax.dot_general(qc, k, (((1,), (1,)), ((), ())), preferred_element_type=F32)
        p = jnp.exp(s - jnp.max(s, axis=1, keepdims=True))
        ws.append((p, jnp.sum(p, axis=1, keepdims=True)))
    w = ws[0][0] * (1.0 / ws[0][1]) - ws[1][0] * (lam / ws[1][1])
    o = jnp.dot(w.astype(BF16), v, preferred_element_type=F32)
    o = o * lax.rsqrt(jnp.mean(o * o, axis=1, keepdims=True) + RMS_EPS) * sg_ref[...]
    o_ref[...] = (o * (1.0 - lam_init)).astype(BF16)


def _attn(q, k, v, lam_qk, sub_g, lam_init):
    bsz, seq, _ = q.shape
    return pl.pallas_call(
        functools.partial(_attn_kernel, lam_init=lam_init),
        grid=(bsz, ATT_HEADS, seq // TQ),
        in_specs=[pl.BlockSpec((None, TQ, ATT_V_DIM), lambda bb, h, i: (bb, i, h)),
                  pl.BlockSpec((None, seq, ATT_V_DIM), lambda bb, h, i: (bb, 0, h)),
                  pl.BlockSpec((None, seq, ATT_V_DIM), lambda bb, h, i: (bb, 0, h)),
                  pl.BlockSpec((4, ATT_QK_DIM), lambda bb, h, i: (0, 0)),
                  pl.BlockSpec((1, ATT_V_DIM), lambda bb, h, i: (0, 0))],
        out_specs=pl.BlockSpec((None, TQ, ATT_V_DIM), lambda bb, h, i: (bb, i, h)),
        out_shape=jax.ShapeDtypeStruct((bsz, seq, ATT_WIDTH), BF16),
        compiler_params=_cparams(("parallel", "parallel", "parallel")),
        name="attn",
    )(q, k, v, lam_qk, sub_g)


def _mix_kernel(att_ref, y2_ref, xs_ref, z_ref, x_ref, dsk_ref, ng_ref, wout_ref, n2g_ref, wr_ref,
                x1_ref, h2_ref, aff_ref):
    y = xs_ref[...] * dsk_ref[...] + y2_ref[0] + y2_ref[1]
    z = z_ref[...]
    y = y * (z * _sigmoid(z))
    gs = SSM_WIDTH // SSM_GROUPS
    acc = jnp.dot(att_ref[...], wout_ref[0:ATT_WIDTH, :], preferred_element_type=F32)
    for g in range(SSM_GROUPS):
        yg = y[:, g * gs:(g + 1) * gs]
        yg = yg * lax.rsqrt(jnp.mean(yg * yg, axis=1, keepdims=True) + RMS_EPS) * ng_ref[:, g * gs:(g + 1) * gs]
        acc = acc + jnp.dot(yg.astype(BF16), wout_ref[ATT_WIDTH + g * gs:ATT_WIDTH + (g + 1) * gs, :],
                            preferred_element_type=F32)
    x1 = x_ref[...] + acc
    x1_ref[...] = x1
    h2 = (x1 * lax.rsqrt(jnp.mean(x1 * x1, axis=1, keepdims=True) + RMS_EPS) * n2g_ref[...]).astype(BF16)
    h2_ref[:, 0:D_MODEL] = h2
    logits = jnp.dot(h2, wr_ref[...], preferred_element_type=F32)
    lane = lax.broadcasted_iota(jnp.int32, logits.shape, 1)
    logits = jnp.where(lane < N_EXPERTS, logits, -jnp.inf)
    e = jnp.exp(logits - jnp.max(logits, axis=1, keepdims=True))
    aff = e / jnp.sum(e, axis=1, keepdims=True)
    aff_ref[...] = aff.T[0:N_EXPERTS, :]
    hi = aff.astype(BF16).astype(F32)
    r1 = aff - hi
    mid = r1.astype(BF16).astype(F32)
    lo = (r1 - mid).astype(BF16).astype(F32)
    aug = hi + pltpu.roll(mid, N_EXPERTS, 1) + pltpu.roll(lo, 2 * N_EXPERTS, 1)
    h2_ref[:, D_MODEL:XE_COLS] = aug.astype(BF16)


def _mixout(att, y2, xs, z, xf, dsk, ng, wout, n2g, wr):
    t = xf.shape[0]
    row = lambda i: (i, 0)
    const = lambda i: (0, 0)
    return pl.pallas_call(
        _mix_kernel,
        grid=(t // TM,),
        in_specs=[pl.BlockSpec((TM, ATT_WIDTH), row),
                  pl.BlockSpec((2, TM, SSM_WIDTH), lambda i: (0, i, 0)),
                  pl.BlockSpec((TM, SSM_WIDTH), row),
                  pl.BlockSpec((TM, SSM_WIDTH), row),
                  pl.BlockSpec((TM, D_MODEL), row),
                  pl.BlockSpec((1, SSM_WIDTH), const),
                  pl.BlockSpec((1, SSM_WIDTH), const),
                  pl.BlockSpec((D_MODEL, D_MODEL), const),
                  pl.BlockSpec((1, D_MODEL), const),
                  pl.BlockSpec((D_MODEL, LANES), const)],
        out_specs=[pl.BlockSpec((TM, D_MODEL), row),
                   pl.BlockSpec((TM, XE_COLS), row),
                   pl.BlockSpec((N_EXPERTS, TM), lambda i: (0, i))],
        out_shape=[jax.ShapeDtypeStruct((t, D_MODEL), F32),
                   jax.ShapeDtypeStruct((t, XE_COLS), BF16),
                   jax.ShapeDtypeStruct((N_EXPERTS, t), F32)],
        compiler_params=_cparams(("parallel",)),
        name="mixout",
    )(att, y2, xs, z, xf, dsk, ng, wout, n2g, wr)


def _select_kernel(aff_ref, sel_ref, pos_ref, *, cap):
    t = aff_ref.shape[1]
    aff = aff_ref[...]
    capf = jnp.float32(cap)

    def moving(c):
        it, _, _, moved = c
        return jnp.logical_and(it < BISECT_MAX_STEPS, moved > 0.0)

    def halve(c):
        it, lo, hi, _ = c
        mid = 0.5 * (lo + hi)
        ge = jnp.sum((aff >= mid).astype(F32), axis=1, keepdims=True) >= capf
        nlo = jnp.where(ge, mid, lo)
        nhi = jnp.where(ge, hi, mid)
        moved = jnp.sum(jnp.logical_or(nlo != lo, nhi != hi).astype(F32))
        return it + 1, nlo, nhi, moved

    _, lo, hi, _ = lax.while_loop(
        moving, halve,
        (jnp.int32(0), jnp.zeros((N_EXPERTS, 1), F32), jnp.full((N_EXPERTS, 1), 2.0, F32), jnp.float32(1.0)))
    need = capf - jnp.sum((aff >= hi).astype(F32), axis=1, keepdims=True)

    r = lax.broadcasted_iota(jnp.int32, (LANES, LANES), 0)
    cidx = lax.broadcasted_iota(jnp.int32, (LANES, LANES), 1)
    before = (r < cidx).astype(BF16)

    def block(cb, carry):
        c_eq, c_sel = carry
        off = pl.multiple_of(cb * LANES, LANES)
        bb = aff_ref[:, pl.ds(off, LANES)]
        gt = bb >= hi
        eq = jnp.logical_and(bb >= lo, bb < hi)
        eqf = eq.astype(F32)
        rank_eq = c_eq + jnp.dot(eqf.astype(BF16), before, preferred_element_type=F32)
        sel = jnp.logical_or(gt, jnp.logical_and(eq, rank_eq < need))
        self_ = sel.astype(F32)
        pos = c_sel + jnp.dot(self_.astype(BF16), before, preferred_element_type=F32)
        pos_ref[:, pl.ds(off, LANES)] = pos.astype(jnp.int32)
        sel_ref[:, pl.ds(off, LANES)] = self_
        return (c_eq + jnp.sum(eqf, axis=1, keepdims=True), c_sel + jnp.sum(self_, axis=1, keepdims=True))

    zero = jnp.zeros((N_EXPERTS, 1), F32)
    lax.fori_loop(0, t // LANES, block, (zero, zero))


def _select(aff, cap):
    t = aff.shape[1]
    full = pl.BlockSpec((N_EXPERTS, t), lambda: (0, 0))
    return pl.pallas_call(
        functools.partial(_select_kernel, cap=cap),
        in_specs=[full],
        out_specs=[full, full],
        out_shape=[jax.ShapeDtypeStruct((N_EXPERTS, t), F32),
                   jax.ShapeDtypeStruct((N_EXPERTS, t), jnp.int32)],
        compiler_params=pltpu.CompilerParams(vmem_limit_bytes=VMEM_LIMIT),
        name="select",
    )(aff)


def _windows(starts_ref, b, nb):
    wins = []
    for e in range(N_EXPERTS):
        s0 = starts_ref[e * (nb + 1) + b]
        s1 = starts_ref[e * (nb + 1) + b + 1]
        a0 = (s0 // ROW_ALIGN) * ROW_ALIGN
        wins.append((a0, s0 - a0, s1 - s0, s1))
    return wins


def _fits_fast(wins):
    ok = None
    for (_, off, n, _) in wins:
        f = off + n <= NP_FAST
        ok = f if ok is None else jnp.logical_and(ok, f)
    return ok


def _one_hot(sel_ref, pos_ref, wins, experts, npw):
    j = lax.broadcasted_iota(jnp.int32, (npw, TB), 0)
    pieces = []
    for e in experts:
        rel = pos_ref[e:e + 1, :] - wins[e][0]
        hit = jnp.logical_and(sel_ref[e:e + 1, :] > 0.0, rel == j)
        pieces.append(hit.astype(BF16))
    return jnp.concatenate(pieces, axis=0)


def _gather_kernel(starts_ref, h2_ref, sel_ref, pos_ref, xe_ref, res_ref, stage_ref, carry_ref, sems, *, nb, cap):
    b = pl.program_id(0)

    @pl.when(b == 0)
    def _():
        carry_ref[...] = jnp.zeros_like(carry_ref)
        res_ref[...] = jnp.zeros_like(res_ref)

    wins = _windows(starts_ref, b, nb)

    def run(experts, npw):
        rows = len(experts) * npw
        oh = _one_hot(sel_ref, pos_ref, wins, experts, npw)
        res_ref[0:rows, :] = jnp.dot(oh, h2_ref[...], preferred_element_type=F32)
        copies = []
        for i, e in enumerate(experts):
            a0, off, n, s1 = wins[e]
            base = i * npw
            res_ref[base:base + ROW_ALIGN, :] = res_ref[base:base + ROW_ALIGN, :] + carry_ref[e]
            stage_ref[base:base + npw, :] = res_ref[base:base + npw, :].astype(BF16)
            nxt = pl.multiple_of((s1 // ROW_ALIGN) * ROW_ALIGN - a0, ROW_ALIGN)
            carry_ref[e] = jnp.where(nxt < npw, res_ref[pl.ds(base + nxt, ROW_ALIGN), :], 0.0)
            cp = pltpu.make_async_copy(stage_ref.at[pl.ds(base, npw)],
                                       xe_ref.at[e, pl.ds(pl.multiple_of(a0, ROW_ALIGN), npw)],
                                       sems.at[e])
            cp.start()
            copies.append(cp)
        for cp in copies:
            cp.wait()

    fast = _fits_fast(wins)

    @pl.when(fast)
    def _():
        run(list(range(N_EXPERTS)), NP_FAST)

    @pl.when(jnp.logical_not(fast))
    def _():
        for g in range(N_EXPERTS // SLOW_GROUP):
            run(list(range(g * SLOW_GROUP, (g + 1) * SLOW_GROUP)), NP_SLOW)

    @pl.when(b == nb - 1)
    def _():
        stage_ref[0:SLOT_PAD, :] = jnp.zeros((SLOT_PAD, XE_COLS), BF16)
        copies = [pltpu.make_async_copy(stage_ref.at[pl.ds(0, SLOT_PAD)], xe_ref.at[e, pl.ds(cap, SLOT_PAD)], sems.at[e])
                  for e in range(N_EXPERTS)]
        for cp in copies:
            cp.start()
        for cp in copies:
            cp.wait()


def _gather(starts, h2aug, sel, pos, cap):
    t = h2aug.shape[0]
    nb = t // TB
    res_rows = max(N_EXPERTS * NP_FAST, SLOW_GROUP * NP_SLOW) + ROW_ALIGN
    grid_spec = pltpu.PrefetchScalarGridSpec(
        num_scalar_prefetch=1,
        grid=(nb,),
        in_specs=[pl.BlockSpec((TB, XE_COLS), lambda b, s: (b, 0)),
                  pl.BlockSpec((N_EXPERTS, TB), lambda b, s: (0, b)),
                  pl.BlockSpec((N_EXPERTS, TB), lambda b, s: (0, b))],
        out_specs=pl.BlockSpec(memory_space=pl.ANY),
        scratch_shapes=[pltpu.VMEM((res_rows, XE_COLS), F32),
                        pltpu.VMEM((res_rows, XE_COLS), BF16),
                        pltpu.VMEM((N_EXPERTS, ROW_ALIGN, XE_COLS), F32),
                        pltpu.SemaphoreType.DMA((N_EXPERTS,))],
    )
    return pl.pallas_call(
        functools.partial(_gather_kernel, nb=nb, cap=cap),
        grid_spec=grid_spec,
        out_shape=jax.ShapeDtypeStruct((N_EXPERTS, cap + SLOT_PAD, XE_COLS), BF16),
        compiler_params=_cparams(("arbitrary",)),
        name="gather",
    )(starts, h2aug, sel, pos)


def _ffn_kernel(xe_ref, wg_ref, wu_ref, wd_ref, y_ref, *, nvalid):
    e = pl.program_id(0)
    i = pl.program_id(1)

    @pl.when(i < nvalid)
    def _():
        x = xe_ref[:, 0:D_MODEL]
        aug = xe_ref[:, D_MODEL:XE_COLS].astype(F32)
        lane = lax.broadcasted_iota(jnp.int32, aug.shape, 1)
        mine = jnp.logical_or(lane == e, jnp.logical_or(lane == e + N_EXPERTS, lane == e + 2 * N_EXPERTS))
        gate = jnp.sum(jnp.where(mine, aug, 0.0), axis=1, keepdims=True)
        g = jnp.dot(x, wg_ref[...], preferred_element_type=F32)
        u = jnp.dot(x, wu_ref[...], preferred_element_type=F32)
        hid = ((g * _sigmoid(g)) * u).astype(BF16)
        y = jnp.dot(hid, wd_ref[...], preferred_element_type=F32) * gate
        y_ref[...] = y.astype(BF16)

    @pl.when(i >= nvalid)
    def _():
        y_ref[...] = jnp.zeros_like(y_ref)


def _ffn(xe, wg, wu, wd, cap):
    rows = xe.shape[1]
    tmf = min(512, cap)
    nvalid = cap // tmf
    wspec = pl.BlockSpec((None, D_MODEL, EXPERT_FF), lambda e, i: (e, 0, 0))
    return pl.pallas_call(
        functools.partial(_ffn_kernel, nvalid=nvalid),
        grid=(N_EXPERTS, rows // tmf),
        in_specs=[pl.BlockSpec((None, tmf, XE_COLS), lambda e, i: (e, jnp.minimum(i, nvalid - 1), 0)),
                  wspec, wspec,
                  pl.BlockSpec((None, EXPERT_FF, D_MODEL), lambda e, i: (e, 0, 0))],
        out_specs=pl.BlockSpec((None, tmf, D_MODEL), lambda e, i: (e, i, 0)),
        out_shape=jax.ShapeDtypeStruct((N_EXPERTS, rows, D_MODEL), BF16),
        compiler_params=_cparams(("parallel", "arbitrary")),
        name="ffn",
    )(xe, wg, wu, wd)


def _combine_kernel(starts_ref, x1_ref, sel_ref, pos_ref, fg_ref, yg_ref, out_ref, stage_ref, sems, *, nb):
    b = pl.program_id(0)
    wins = _windows(starts_ref, b, nb)

    def run(experts, npw):
        rows = len(experts) * npw
        copies = []
        for i, e in enumerate(experts):
            cp = pltpu.make_async_copy(yg_ref.at[e, pl.ds(pl.multiple_of(wins[e][0], ROW_ALIGN), npw)],
                                       stage_ref.at[pl.ds(i * npw, npw)],
                                       sems.at[e])
            cp.start()
            copies.append(cp)
        oh = _one_hot(sel_ref, pos_ref, wins, experts, npw)
        for cp in copies:
            cp.wait()
        return lax.dot_general(oh, stage_ref[0:rows, :], (((0,), (0,)), ((), ())),
                               preferred_element_type=F32)

    def finish(moe):
        x = x1_ref[...] + moe
        out_ref[...] = x * lax.rsqrt(jnp.mean(x * x, axis=1, keepdims=True) + RMS_EPS) * fg_ref[...]

    fast = _fits_fast(wins)

    @pl.when(fast)
    def _():
        finish(run(list(range(N_EXPERTS)), NP_FAST))

    @pl.when(jnp.logical_not(fast))
    def _():
        moe = jnp.zeros((TB, D_MODEL), F32)
        for g in range(N_EXPERTS // SLOW_GROUP):
            moe = moe + run(list(range(g * SLOW_GROUP, (g + 1) * SLOW_GROUP)), NP_SLOW)
        finish(moe)


def _combine(starts, x1, sel, pos, fg, yg):
    t = x1.shape[0]
    nb = t // TB
    stage_rows = max(N_EXPERTS * NP_FAST, SLOW_GROUP * NP_SLOW)
    grid_spec = pltpu.PrefetchScalarGridSpec(
        num_scalar_prefetch=1,
        grid=(nb,),
        in_specs=[pl.BlockSpec((TB, D_MODEL), lambda b, s: (b, 0)),
                  pl.BlockSpec((N_EXPERTS, TB), lambda b, s: (0, b)),
                  pl.BlockSpec((N_EXPERTS, TB), lambda b, s: (0, b)),
                  pl.BlockSpec((1, D_MODEL), lambda b, s: (0, 0)),
                  pl.BlockSpec(memory_space=pl.ANY)],
        out_specs=pl.BlockSpec((TB, D_MODEL), lambda b, s: (b, 0)),
        scratch_shapes=[pltpu.VMEM((stage_rows, D_MODEL), BF16),
                        pltpu.SemaphoreType.DMA((N_EXPERTS,))],
    )
    return pl.pallas_call(
        functools.partial(_combine_kernel, nb=nb),
        grid_spec=grid_spec,
        out_shape=jax.ShapeDtypeStruct((t, D_MODEL), F32),
        compiler_params=_cparams(("arbitrary",)),
        name="combine",
    )(starts, x1, sel, pos, fg, yg)


def _rope_tables(seq):
    half = ROPE_DIM // 2
    inv_freq = jnp.power(ROPE_THETA, -jnp.arange(half, dtype=F32) * 2.0 / ROPE_DIM)
    ang = jnp.arange(seq).astype(F32)[:, None] * inv_freq[None, :]
    cos, sin = jnp.cos(ang), jnp.sin(ang)
    ones = jnp.ones((seq, ATT_QK_DIM - ROPE_DIM), F32)
    zeros = jnp.zeros((seq, ATT_QK_DIM - ROPE_DIM), F32)
    z8 = jnp.zeros((seq, half), F32)
    cos64 = jnp.concatenate([cos, cos, ones], axis=1)
    sa64 = jnp.concatenate([z8, sin, zeros], axis=1)
    sb64 = jnp.concatenate([-sin, z8, zeros], axis=1)
    tile = lambda a: jnp.concatenate([a, a], axis=1)
    return tile(cos64), tile(sa64), tile(sb64)


def _lane_row(v):
    return jnp.pad(v.astype(F32), ((0, 0), (0, LANES - v.shape[1])))[:, None, :]


def _trunk(x, p, lam_init):
    bsz, seq, _ = x.shape
    t = bsz * seq
    cap = CAPACITY_FACTOR * t // N_EXPERTS
    nb = t // TB
    assert seq % TM == 0 and seq % TC == 0 and seq % TQ == 0 and t % TB == 0
    assert cap % min(512, cap) == 0 and cap % ROW_ALIGN == 0 and SLOT_PAD % min(512, cap) == 0

    xf = x.reshape(t, D_MODEL)
    q, k, v, z, xbc, dt2 = _inproj(xf, p["norm1_g"], p["w_in"], *p["rope"](seq), seq)
    xs, bc = _conv(xbc.reshape(bsz, seq, XBC_COLS), p["conv_w"], p["conv_b"])
    y2 = _ssd(xs, bc, dt2.reshape(2, bsz, seq, LANES), p["dt_bias"], p["a_log"])
    att = _attn(q.reshape(bsz, seq, 512), k.reshape(bsz, seq, 512), v.reshape(bsz, seq, 512),
                p["lam_qk"], p["attn_sub_g"], lam_init)
    x1, h2aug, aff = _mixout(att.reshape(t, ATT_WIDTH), y2.reshape(2, t, SSM_WIDTH), xs.reshape(t, SSM_WIDTH),
                             z, xf, p["d_skip"], p["ssm_norm_g"], p["w_out"], p["norm2_g"], p["w_router"])
    sel, pos = _select(aff, cap)
    starts = jnp.concatenate([pos[:, ::TB].astype(jnp.int32),
                              jnp.full((N_EXPERTS, 1), cap, jnp.int32)], axis=1).reshape(-1)
    xe = _gather(starts, h2aug, sel, pos, cap)
    yg = _ffn(xe, p["w_gate"], p["w_up"], p["w_down"], cap)
    out = _combine(starts, x1, sel, pos, p["final_g"], yg)
    return out.reshape(bsz, seq, D_MODEL)


def kernel(x_prompt, x_sample, norm1_g, w_in, lam_qk, attn_sub_g, conv_w, conv_b, dt_bias, a_log, d_skip,
           ssm_norm_g, w_out, norm2_g, w_router, w_gate, w_up, w_down, final_g):
    depth = w_in.shape[0]
    assert depth == 1
    l = 0
    lam_init = 0.8 - 0.6 * math.exp(-0.3 * l)
    w = w_in[l]
    dt_cols = w[:, 3072:3088]
    pad = jnp.zeros((D_MODEL, LANES - SSM_HEADS), w.dtype)
    w_pad = jnp.concatenate([w[:, :3072], dt_cols[:, :SSM_HEADS], pad, dt_cols[:, SSM_HEADS:], pad], axis=1)
    tables = {}

    def rope(seq):
        if seq not in tables:
            tables[seq] = _rope_tables(seq)
        return tables[seq]

    p = {
        "norm1_g": norm1_g[l][None, :],
        "w_in": w_pad.astype(BF16),
        "rope": rope,
        "lam_qk": lam_qk[l],
        "attn_sub_g": attn_sub_g[l][None, :],
        "conv_w": conv_w[l],
        "conv_b": conv_b[l][None, :],
        "dt_bias": _lane_row(dt_bias[l]),
        "a_log": _lane_row(a_log[l]),
        "d_skip": jnp.repeat(d_skip[l], SSM_HEAD_DIM)[None, :],
        "ssm_norm_g": ssm_norm_g[l][None, :],
        "w_out": w_out[l].astype(BF16),
        "norm2_g": norm2_g[l][None, :],
        "w_router": jnp.pad(w_router[l], ((0, 0), (0, LANES - N_EXPERTS))).astype(BF16),
        "w_gate": w_gate[l].astype(BF16),
        "w_up": w_up[l].astype(BF16),
        "w_down": w_down[l].astype(BF16),
        "final_g": final_g[None, :],
    }
    return (_trunk(x_prompt, p, lam_init), _trunk(x_sample, p, lam_init))
```

```python
import functools
import math

import jax
import jax.numpy as jnp
from jax import lax
from jax.experimental import pallas as pl
from jax.experimental.pallas import tpu as pltpu

F32 = jnp.float32
BF16 = jnp.bfloat16
HIGHEST = lax.Precision.HIGHEST

D_MODEL = 1024
RMS_EPS = 1e-6
ATT_HEADS = 4
ATT_QK_DIM = 64
ATT_V_DIM = 128
ATT_WIDTH = 512
ROPE_DIM = 16
ROPE_THETA = 500000.0
SSM_HEADS = 8
SSM_HEAD_DIM = 64
SSM_WIDTH = 512
SSM_GROUPS = 2
SSM_STATE = 128
CONV_WIDTH = 5
CHUNK = 128
XBC_COLS = 1024
N_EXPERTS = 16
EXPERT_FF = 1024
CAPACITY_FACTOR = 2

LANES = 128
ROW_ALIGN = 16
IN_PAD = 3328
AUG = 128
XE_COLS = D_MODEL + AUG

TM = 512
TQ = 256
TC = 512
TB = 256
NP_FAST = 64
NP_SLOW = TB + ROW_ALIGN
SLOW_GROUP = 4
SLOT_PAD = 512
BISECT_MAX_STEPS = 200
VMEM_LIMIT = 56 * 1024 * 1024


def _cparams(sem):
    return pltpu.CompilerParams(dimension_semantics=sem, vmem_limit_bytes=VMEM_LIMIT)


def _sigmoid(x):
    return 1.0 / (1.0 + jnp.exp(-x))


def _inproj_kernel(x_ref, g_ref, w_ref, cos_ref, sa_ref, sb_ref,
                   q_ref, k_ref, v_ref, z_ref, xbc_ref, dt_ref):
    x = x_ref[...]
    ms = jnp.mean(x * x, axis=-1, keepdims=True)
    h = (x * lax.rsqrt(ms + RMS_EPS) * g_ref[...]).astype(BF16)

    def proj(lo, hi):
        return jnp.dot(h, w_ref[:, lo:hi], preferred_element_type=F32)

    cos = cos_ref[...]
    sa = sa_ref[...]
    sb = sb_ref[...]
    for ref, off, scale in ((q_ref, 0, ATT_QK_DIM ** -0.5 * math.log2(math.e)), (k_ref, 512, 1.0)):
        for c in range(4):
            t = proj(off + c * LANES, off + (c + 1) * LANES)
            r = t * cos + pltpu.roll(t, 8, 1) * sa + pltpu.roll(t, LANES - 8, 1) * sb
            ref[:, c * LANES:(c + 1) * LANES] = (r * scale).astype(BF16)
    v_ref[...] = proj(1024, 1536).astype(BF16)
    z_ref[...] = proj(1536, 2048)
    xbc_ref[...] = proj(2048, 3072)
    dt_ref[0] = proj(3072, 3200)
    dt_ref[1] = proj(3200, 3328)


def _inproj(xf, g, w, cos_t, sa_t, sb_t, seq):
    t = xf.shape[0]
    nps = seq // TM
    row = lambda i: (i, 0)
    pos = lambda i: (i % nps, 0)
    const = lambda i: (0, 0)
    return pl.pallas_call(
        _inproj_kernel,
        grid=(t // TM,),
        in_specs=[pl.BlockSpec((TM, D_MODEL), row),
                  pl.BlockSpec((1, D_MODEL), const),
                  pl.BlockSpec((D_MODEL, IN_PAD), const),
                  pl.BlockSpec((TM, LANES), pos),
                  pl.BlockSpec((TM, LANES), pos),
                  pl.BlockSpec((TM, LANES), pos)],
        out_specs=[pl.BlockSpec((TM, 512), row),
                   pl.BlockSpec((TM, 512), row),
                   pl.BlockSpec((TM, 512), row),
                   pl.BlockSpec((TM, 512), row),
                   pl.BlockSpec((TM, XBC_COLS), row),
                   pl.BlockSpec((2, TM, LANES), lambda i: (0, i, 0))],
        out_shape=[jax.ShapeDtypeStruct((t, 512), BF16),
                   jax.ShapeDtypeStruct((t, 512), BF16),
                   jax.ShapeDtypeStruct((t, 512), BF16),
                   jax.ShapeDtypeStruct((t, 512), F32),
                   jax.ShapeDtypeStruct((t, XBC_COLS), F32),
                   jax.ShapeDtypeStruct((2, t, LANES), F32)],
        compiler_params=_cparams(("parallel",)),
        name="inproj",
    )(xf, g, w, cos_t, sa_t, sb_t)


def _conv_kernel(x_ref, p_ref, n_ref, w_ref, b_ref, xs_ref, bc_ref, buf_ref):
    i = pl.program_id(1)
    last = pl.num_programs(1) - 1
    buf_ref[0:8, :] = jnp.where(i > 0, p_ref[...], 0.0)
    buf_ref[8:8 + TC, :] = x_ref[...]
    buf_ref[8 + TC:16 + TC, :] = jnp.where(i < last, n_ref[...], 0.0)
    acc = jnp.broadcast_to(b_ref[...], (TC, XBC_COLS))
    for j in range(CONV_WIDTH):
        acc = acc + w_ref[j:j + 1, :] * buf_ref[pl.ds(8 - CONV_WIDTH // 2 + j, TC), :]
    out = acc * _sigmoid(acc)
    xs_ref[...] = out[:, :SSM_WIDTH]
    bc_ref[...] = out[:, SSM_WIDTH:].astype(BF16)


def _conv(xbc, w, b):
    bsz, seq, _ = xbc.shape
    nt = seq // TC
    hb = TC // 8
    return pl.pallas_call(
        _conv_kernel,
        grid=(bsz, nt),
        in_specs=[pl.BlockSpec((None, TC, XBC_COLS), lambda bb, i: (bb, i, 0)),
                  pl.BlockSpec((None, 8, XBC_COLS), lambda bb, i: (bb, jnp.maximum(i * hb - 1, 0), 0)),
                  pl.BlockSpec((None, 8, XBC_COLS), lambda bb, i: (bb, jnp.minimum((i + 1) * hb, seq // 8 - 1), 0)),
                  pl.BlockSpec((CONV_WIDTH, XBC_COLS), lambda bb, i: (0, 0)),
                  pl.BlockSpec((1, XBC_COLS), lambda bb, i: (0, 0))],
        out_specs=[pl.BlockSpec((None, TC, SSM_WIDTH), lambda bb, i: (bb, i, 0)),
                   pl.BlockSpec((None, TC, 512), lambda bb, i: (bb, i, 0))],
        out_shape=[jax.ShapeDtypeStruct((bsz, seq, SSM_WIDTH), F32),
                   jax.ShapeDtypeStruct((bsz, seq, 512), BF16)],
        scratch_shapes=[pltpu.VMEM((TC + 16, XBC_COLS), F32)],
        compiler_params=_cparams(("parallel", "parallel")),
        name="conv",
    )(xbc, xbc, xbc, w, b)


def _ssd_kernel(xs_ref, bc_ref, dt_ref, dtb_ref, alog_ref, y_ref, st_ref):
    d = pl.program_id(0)
    c = pl.program_id(2)

    @pl.when(c == 0)
    def _():
        st_ref[...] = jnp.zeros_like(st_ref)

    L = CHUNK
    row = lax.broadcasted_iota(jnp.int32, (L, L), 0)
    col = lax.broadcasted_iota(jnp.int32, (L, L), 1)
    mask = (row - col) * (1 - 2 * d) >= 0
    tri = mask.astype(F32)

    raw = dt_ref[...] + dtb_ref[...]
    dt = jnp.maximum(raw, 0.0) + jnp.log(1.0 + jnp.exp(-jnp.abs(raw)))
    a = dt * (-jnp.exp(alog_ref[...]))
    cs_col = jnp.dot(tri, a, preferred_element_type=F32, precision=HIGHEST)
    cs_row = lax.dot_general(a, tri, (((0,), (1,)), ((), ())),
                             preferred_element_type=F32, precision=HIGHEST)
    tot = jnp.sum(a, axis=0, keepdims=True)
    e_cs = jnp.exp(cs_col)
    e_dec = jnp.exp(tot - cs_col)
    e_tot = jnp.exp(tot)

    xs = xs_ref[...]
    bc = bc_ref[...]
    ys = []
    for g in range(SSM_GROUPS):
        bg = bc[:, g * SSM_STATE:(g + 1) * SSM_STATE]
        cg = bc[:, 256 + g * SSM_STATE:256 + (g + 1) * SSM_STATE]
        cb = lax.dot_general(cg, bg, (((1,), (1,)), ((), ())), preferred_element_type=F32)
        for r in range(SSM_HEADS // SSM_GROUPS):
            h = g * (SSM_HEADS // SSM_GROUPS) + r
            csc = cs_col[:, h:h + 1]
            csr = cs_row[h:h + 1, :]
            lm = jnp.exp(jnp.where(mask, csc - csr, -jnp.inf))
            mh = (cb * lm).astype(BF16)
            xdt = xs[:, h * SSM_HEAD_DIM:(h + 1) * SSM_HEAD_DIM] * dt[:, h:h + 1]
            y_diag = jnp.dot(mh, xdt.astype(BF16), preferred_element_type=F32)
            s_prev = st_ref[h]
            y_off = jnp.dot(cg, s_prev.astype(BF16), preferred_element_type=F32) * e_cs[:, h:h + 1]
            xdw = (xdt * e_dec[:, h:h + 1]).astype(BF16)
            s_new = lax.dot_general(bg, xdw, (((0,), (0,)), ((), ())), preferred_element_type=F32)
            st_ref[h] = s_prev * e_tot[:, h:h + 1] + s_new
            ys.append(y_diag + y_off)
    y_ref[...] = jnp.concatenate(ys, axis=1)


def _ssd(xs, bc, dt2, dtb, alog):
    bsz, seq, _ = xs.shape
    nc = seq // CHUNK
    chunk = lambda d, bb, c: (bb, c + d * (nc - 1 - 2 * c), 0)
    return pl.pallas_call(
        _ssd_kernel,
        grid=(2, bsz, nc),
        in_specs=[pl.BlockSpec((None, CHUNK, SSM_WIDTH), chunk),
                  pl.BlockSpec((None, CHUNK, 512), chunk),
                  pl.BlockSpec((None, None, CHUNK, LANES), lambda d, bb, c: (d, bb, c + d * (nc - 1 - 2 * c), 0)),
                  pl.BlockSpec((None, 1, LANES), lambda d, bb, c: (d, 0, 0)),
                  pl.BlockSpec((None, 1, LANES), lambda d, bb, c: (d, 0, 0))],
        out_specs=pl.BlockSpec((None, None, CHUNK, SSM_WIDTH),
                               lambda d, bb, c: (d, bb, c + d * (nc - 1 - 2 * c), 0)),
        out_shape=jax.ShapeDtypeStruct((2, bsz, seq, SSM_WIDTH), F32),
        scratch_shapes=[pltpu.VMEM((SSM_HEADS, SSM_STATE, SSM_HEAD_DIM), F32)],
        compiler_params=_cparams(("arbitrary", "arbitrary", "arbitrary")),
        name="ssd",
    )(xs, bc, dt2, dtb, alog)


def _attn_kernel(q_ref, k_ref, v_ref, lq_ref, sg_ref, o_ref, *, lam_init):
    lq = lq_ref[...]
    lam = (jnp.exp(jnp.sum(lq[0:1] * lq[1:2], axis=1, keepdims=True))
           - jnp.exp(jnp.sum(lq[2:3] * lq[3:4], axis=1, keepdims=True)) + lam_init)
    q = q_ref[...]
    k = k_ref[...]
    v = v_ref[...]
    lane = lax.broadcasted_iota(jnp.int32, q.shape, 1)
    zero = jnp.zeros_like(q)
    ws = []
    for comp in range(2):
        keep = (lane < ATT_QK_DIM) if comp == 0 else (lane >= ATT_QK_DIM)
        qc = jnp.where(keep, q, zero)
        s = lax.dot_general(qc, k, (((1,), (1,)), ((), ())), preferred_element_type=F32)
        p = jnp.exp2(s - jnp.max(s, axis=1, keepdims=True))
        ws.append((p, jnp.sum(p, axis=1, keepdims=True)))
    w = ws[0][0] - ws[1][0] * (lam * ws[0][1] / ws[1][1])
    o = jnp.dot(w.astype(BF16), v, preferred_element_type=F32) * (1.0 / ws[0][1])
    o = o * lax.rsqrt(jnp.mean(o * o, axis=1, keepdims=True) + RMS_EPS) * sg_ref[...]
    o_ref[...] = (o * (1.0 - lam_init)).astype(BF16)


def _attn(q, k, v, lam_qk, sub_g, lam_init):
    bsz, seq, _ = q.shape
    return pl.pallas_call(
        functools.partial(_attn_kernel, lam_init=lam_init),
        grid=(bsz, ATT_HEADS, seq // TQ),
        in_specs=[pl.BlockSpec((None, TQ, ATT_V_DIM), lambda bb, h, i: (bb, i, h)),
                  pl.BlockSpec((None, seq, ATT_V_DIM), lambda bb, h, i: (bb, 0, h)),
                  pl.BlockSpec((None, seq, ATT_V_DIM), lambda bb, h, i: (bb, 0, h)),
                  pl.BlockSpec((4, ATT_QK_DIM), lambda bb, h, i: (0, 0)),
                  pl.BlockSpec((1, ATT_V_DIM), lambda bb, h, i: (0, 0))],
        out_specs=pl.BlockSpec((None, TQ, ATT_V_DIM), lambda bb, h, i: (bb, i, h)),
        out_shape=jax.ShapeDtypeStruct((bsz, seq, ATT_WIDTH), BF16),
        compiler_params=_cparams(("parallel", "parallel", "parallel")),
        name="attn",
    )(q, k, v, lam_qk, sub_g)


def _mix_kernel(att_ref, y2_ref, xs_ref, z_ref, x_ref, dsk_ref, ng_ref, wout_ref, n2g_ref, wr_ref,
                x1_ref, h2_ref, aff_ref):
    y = xs_ref[...] * dsk_ref[...] + y2_ref[0] + y2_ref[1]
    z = z_ref[...]
    y = y * (z * _sigmoid(z))
    gs = SSM_WIDTH // SSM_GROUPS
    acc = jnp.dot(att_ref[...], wout_ref[0:ATT_WIDTH, :], preferred_element_type=F32)
    for g in range(SSM_GROUPS):
        yg = y[:, g * gs:(g + 1) * gs]
        yg = yg * lax.rsqrt(jnp.mean(yg * yg, axis=1, keepdims=True) + RMS_EPS) * ng_ref[:, g * gs:(g + 1) * gs]
        acc = acc + jnp.dot(yg.astype(BF16), wout_ref[ATT_WIDTH + g * gs:ATT_WIDTH + (g + 1) * gs, :],
                            preferred_element_type=F32)
    x1 = x_ref[...] + acc
    x1_ref[...] = x1
    h2 = (x1 * lax.rsqrt(jnp.mean(x1 * x1, axis=1, keepdims=True) + RMS_EPS) * n2g_ref[...]).astype(BF16)
    h2_ref[:, 0:D_MODEL] = h2
    logits = jnp.dot(h2, wr_ref[...], preferred_element_type=F32)
    lane = lax.broadcasted_iota(jnp.int32, logits.shape, 1)
    logits = jnp.where(lane < N_EXPERTS, logits, -jnp.inf)
    e = jnp.exp(logits - jnp.max(logits, axis=1, keepdims=True))
    aff = e / jnp.sum(e, axis=1, keepdims=True)
    aff_ref[...] = aff.T[0:N_EXPERTS, :]
    hi = aff.astype(BF16).astype(F32)
    r1 = aff - hi
    mid = r1.astype(BF16).astype(F32)
    lo = (r1 - mid).astype(BF16).astype(F32)
    aug = hi + pltpu.roll(mid, N_EXPERTS, 1) + pltpu.roll(lo, 2 * N_EXPERTS, 1)
    h2_ref[:, D_MODEL:XE_COLS] = aug.astype(BF16)


def _mixout(att, y2, xs, z, xf, dsk, ng, wout, n2g, wr):
    t = xf.shape[0]
    row = lambda i: (i, 0)
    const = lambda i: (0, 0)
    return pl.pallas_call(
        _mix_kernel,
        grid=(t // TM,),
        in_specs=[pl.BlockSpec((TM, ATT_WIDTH), row),
                  pl.BlockSpec((2, TM, SSM_WIDTH), lambda i: (0, i, 0)),
                  pl.BlockSpec((TM, SSM_WIDTH), row),
                  pl.BlockSpec((TM, SSM_WIDTH), row),
                  pl.BlockSpec((TM, D_MODEL), row),
                  pl.BlockSpec((1, SSM_WIDTH), const),
                  pl.BlockSpec((1, SSM_WIDTH), const),
                  pl.BlockSpec((D_MODEL, D_MODEL), const),
                  pl.BlockSpec((1, D_MODEL), const),
                  pl.BlockSpec((D_MODEL, LANES), const)],
        out_specs=[pl.BlockSpec((TM, D_MODEL), row),
                   pl.BlockSpec((TM, XE_COLS), row),
                   pl.BlockSpec((N_EXPERTS, TM), lambda i: (0, i))],
        out_shape=[jax.ShapeDtypeStruct((t, D_MODEL), F32),
                   jax.ShapeDtypeStruct((t, XE_COLS), BF16),
                   jax.ShapeDtypeStruct((N_EXPERTS, t), F32)],
        compiler_params=_cparams(("parallel",)),
        name="mixout",
    )(att, y2, xs, z, xf, dsk, ng, wout, n2g, wr)


def _select_kernel(aff_ref, sel_ref, pos_ref, *, cap):
    t = aff_ref.shape[1]
    aff = aff_ref[...]
    capf = jnp.float32(cap)

    def moving(c):
        it, _, _, moved = c
        return jnp.logical_and(it < BISECT_MAX_STEPS, moved > 0.0)

    def halve(c):
        it, lo, hi, _ = c
        mid = 0.5 * (lo + hi)
        ge = jnp.sum((aff >= mid).astype(F32), axis=1, keepdims=True) >= capf
        nlo = jnp.where(ge, mid, lo)
        nhi = jnp.where(ge, hi, mid)
        moved = jnp.sum(jnp.logical_or(nlo != lo, nhi != hi).astype(F32))
        return it + 1, nlo, nhi, moved

    _, lo, hi, _ = lax.while_loop(
        moving, halve,
        (jnp.int32(0), jnp.zeros((N_EXPERTS, 1), F32), jnp.full((N_EXPERTS, 1), 2.0, F32), jnp.float32(1.0)))
    need = capf - jnp.sum((aff >= hi).astype(F32), axis=1, keepdims=True)

    r = lax.broadcasted_iota(jnp.int32, (LANES, LANES), 0)
    cidx = lax.broadcasted_iota(jnp.int32, (LANES, LANES), 1)
    before = (r < cidx).astype(BF16)

    def block(cb, carry):
        c_eq, c_sel = carry
        off = pl.multiple_of(cb * LANES, LANES)
        bb = aff_ref[:, pl.ds(off, LANES)]
        gt = bb >= hi
        eq = jnp.logical_and(bb >= lo, bb < hi)
        eqf = eq.astype(F32)
        rank_eq = c_eq + jnp.dot(eqf.astype(BF16), before, preferred_element_type=F32)
        sel = jnp.logical_or(gt, jnp.logical_and(eq, rank_eq < need))
        self_ = sel.astype(F32)
        pos = c_sel + jnp.dot(self_.astype(BF16), before, preferred_element_type=F32)
        pos_ref[:, pl.ds(off, LANES)] = pos.astype(jnp.int32)
        sel_ref[:, pl.ds(off, LANES)] = self_
        return (c_eq + jnp.sum(eqf, axis=1, keepdims=True), c_sel + jnp.sum(self_, axis=1, keepdims=True))

    zero = jnp.zeros((N_EXPERTS, 1), F32)
    lax.fori_loop(0, t // LANES, block, (zero, zero))


def _select(aff, cap):
    t = aff.shape[1]
    full = pl.BlockSpec((N_EXPERTS, t), lambda: (0, 0))
    return pl.pallas_call(
        functools.partial(_select_kernel, cap=cap),
        in_specs=[full],
        out_specs=[full, full],
        out_shape=[jax.ShapeDtypeStruct((N_EXPERTS, t), F32),
                   jax.ShapeDtypeStruct((N_EXPERTS, t), jnp.int32)],
        compiler_params=pltpu.CompilerParams(vmem_limit_bytes=VMEM_LIMIT),
        name="select",
    )(aff)


def _windows(starts_ref, b, nb):
    wins = []
    for e in range(N_EXPERTS):
        s0 = starts_ref[e * (nb + 1) + b]
        s1 = starts_ref[e * (nb + 1) + b + 1]
        a0 = (s0 // ROW_ALIGN) * ROW_ALIGN
        wins.append((a0, s0 - a0, s1 - s0, s1))
    return wins


def _fits_fast(wins):
    ok = None
    for (_, off, n, _) in wins:
        f = off + n <= NP_FAST
        ok = f if ok is None else jnp.logical_and(ok, f)
    return ok


def _one_hot(sel_ref, pos_ref, wins, experts, npw):
    j = lax.broadcasted_iota(jnp.int32, (npw, TB), 0)
    pieces = []
    for e in experts:
        rel = pos_ref[e:e + 1, :] - wins[e][0]
        hit = jnp.logical_and(sel_ref[e:e + 1, :] > 0.0, rel == j)
        pieces.append(hit.astype(BF16))
    return jnp.concatenate(pieces, axis=0)


def _gather_kernel(starts_ref, h2_ref, sel_ref, pos_ref, xe_ref, res_ref, stage_ref, carry_ref, sems, *, nb, cap):
    b = pl.program_id(0)

    @pl.when(b == 0)
    def _():
        carry_ref[...] = jnp.zeros_like(carry_ref)
        res_ref[...] = jnp.zeros_like(res_ref)

    wins = _windows(starts_ref, b, nb)

    def run(experts, npw):
        rows = len(experts) * npw
        oh = _one_hot(sel_ref, pos_ref, wins, experts, npw)
        res_ref[0:rows, :] = jnp.dot(oh, h2_ref[...], preferred_element_type=F32)
        copies = []
        for i, e in enumerate(experts):
            a0, off, n, s1 = wins[e]
            base = i * npw
            res_ref[base:base + ROW_ALIGN, :] = res_ref[base:base + ROW_ALIGN, :] + carry_ref[e]
            stage_ref[base:base + npw, :] = res_ref[base:base + npw, :].astype(BF16)
            nxt = pl.multiple_of((s1 // ROW_ALIGN) * ROW_ALIGN - a0, ROW_ALIGN)
            carry_ref[e] = jnp.where(nxt < npw, res_ref[pl.ds(base + nxt, ROW_ALIGN), :], 0.0)
            cp = pltpu.make_async_copy(stage_ref.at[pl.ds(base, npw)],
                                       xe_ref.at[e, pl.ds(pl.multiple_of(a0, ROW_ALIGN), npw)],
                                       sems.at[e])
            cp.start()
            copies.append(cp)
        for cp in copies:
            cp.wait()

    fast = _fits_fast(wins)

    @pl.when(fast)
    def _():
        run(list(range(N_EXPERTS)), NP_FAST)

    @pl.when(jnp.logical_not(fast))
    def _():
        for g in range(N_EXPERTS // SLOW_GROUP):
            run(list(range(g * SLOW_GROUP, (g + 1) * SLOW_GROUP)), NP_SLOW)

    @pl.when(b == nb - 1)
    def _():
        stage_ref[0:SLOT_PAD, :] = jnp.zeros((SLOT_PAD, XE_COLS), BF16)
        copies = [pltpu.make_async_copy(stage_ref.at[pl.ds(0, SLOT_PAD)], xe_ref.at[e, pl.ds(cap, SLOT_PAD)], sems.at[e])
                  for e in range(N_EXPERTS)]
        for cp in copies:
            cp.start()
        for cp in copies:
            cp.wait()


def _gather(starts, h2aug, sel, pos, cap):
    t = h2aug.shape[0]
    nb = t // TB
    res_rows = max(N_EXPERTS * NP_FAST, SLOW_GROUP * NP_SLOW) + ROW_ALIGN
    grid_spec = pltpu.PrefetchScalarGridSpec(
        num_scalar_prefetch=1,
        grid=(nb,),
        in_specs=[pl.BlockSpec((TB, XE_COLS), lambda b, s: (b, 0)),
                  pl.BlockSpec((N_EXPERTS, TB), lambda b, s: (0, b)),
                  pl.BlockSpec((N_EXPERTS, TB), lambda b, s: (0, b))],
        out_specs=pl.BlockSpec(memory_space=pl.ANY),
        scratch_shapes=[pltpu.VMEM((res_rows, XE_COLS), F32),
                        pltpu.VMEM((res_rows, XE_COLS), BF16),
                        pltpu.VMEM((N_EXPERTS, ROW_ALIGN, XE_COLS), F32),
                        pltpu.SemaphoreType.DMA((N_EXPERTS,))],
    )
    return pl.pallas_call(
        functools.partial(_gather_kernel, nb=nb, cap=cap),
        grid_spec=grid_spec,
        out_shape=jax.ShapeDtypeStruct((N_EXPERTS, cap + SLOT_PAD, XE_COLS), BF16),
        compiler_params=_cparams(("arbitrary",)),
        name="gather",
    )(starts, h2aug, sel, pos)


def _ffn_kernel(xe_ref, wg_ref, wu_ref, wd_ref, y_ref, *, nvalid):
    e = pl.program_id(0)
    i = pl.program_id(1)

    @pl.when(i < nvalid)
    def _():
        x = xe_ref[:, 0:D_MODEL]
        aug = xe_ref[:, D_MODEL:XE_COLS].astype(F32)
        lane = lax.broadcasted_iota(jnp.int32, aug.shape, 1)
        mine = jnp.logical_or(lane == e, jnp.logical_or(lane == e + N_EXPERTS, lane == e + 2 * N_EXPERTS))
        gate = jnp.sum(jnp.where(mine, aug, 0.0), axis=1, keepdims=True)
        g = jnp.dot(x, wg_ref[...], preferred_element_type=F32)
        u = jnp.dot(x, wu_ref[...], preferred_element_type=F32)
        hid = ((g * _sigmoid(g)) * u).astype(BF16)
        y = jnp.dot(hid, wd_ref[...], preferred_element_type=F32) * gate
        y_ref[...] = y.astype(BF16)

    @pl.when(i >= nvalid)
    def _():
        y_ref[...] = jnp.zeros_like(y_ref)


def _ffn(xe, wg, wu, wd, cap):
    rows = xe.shape[1]
    tmf = min(512, cap)
    nvalid = cap // tmf
    wspec = pl.BlockSpec((None, D_MODEL, EXPERT_FF), lambda e, i: (e, 0, 0))
    return pl.pallas_call(
        functools.partial(_ffn_kernel, nvalid=nvalid),
        grid=(N_EXPERTS, rows // tmf),
        in_specs=[pl.BlockSpec((None, tmf, XE_COLS), lambda e, i: (e, jnp.minimum(i, nvalid - 1), 0)),
                  wspec, wspec,
                  pl.BlockSpec((None, EXPERT_FF, D_MODEL), lambda e, i: (e, 0, 0))],
        out_specs=pl.BlockSpec((None, tmf, D_MODEL), lambda e, i: (e, i, 0)),
        out_shape=jax.ShapeDtypeStruct((N_EXPERTS, rows, D_MODEL), BF16),
        compiler_params=_cparams(("parallel", "arbitrary")),
        name="ffn",
    )(xe, wg, wu, wd)


def _combine_kernel(starts_ref, x1_ref, sel_ref, pos_ref, fg_ref, yg_ref, out_ref,
                    ring_ref, stage_ref, ring_sems, sems, *, nb):
    b = pl.program_id(0)
    slot = b % 2
    wins = _windows(starts_ref, b, nb)
    all_experts = list(range(N_EXPERTS))

    def ring_copies(blk, slt):
        copies = []
        for e in all_experts:
            s0 = starts_ref[e * (nb + 1) + blk]
            a0 = pl.multiple_of((s0 // ROW_ALIGN) * ROW_ALIGN, ROW_ALIGN)
            copies.append(pltpu.make_async_copy(yg_ref.at[e, pl.ds(a0, NP_FAST)],
                                                ring_ref.at[slt, pl.ds(e * NP_FAST, NP_FAST)],
                                                ring_sems.at[slt, e]))
        return copies

    @pl.when(b == 0)
    def _():
        for cp in ring_copies(0, 0):
            cp.start()

    @pl.when(b + 1 < nb)
    def _():
        for cp in ring_copies(b + 1, 1 - slot):
            cp.start()

    for cp in ring_copies(b, slot):
        cp.wait()

    def run_slow(experts):
        rows = len(experts) * NP_SLOW
        copies = []
        for i, e in enumerate(experts):
            cp = pltpu.make_async_copy(yg_ref.at[e, pl.ds(pl.multiple_of(wins[e][0], ROW_ALIGN), NP_SLOW)],
                                       stage_ref.at[pl.ds(i * NP_SLOW, NP_SLOW)],
                                       sems.at[e])
            cp.start()
            copies.append(cp)
        oh = _one_hot(sel_ref, pos_ref, wins, experts, NP_SLOW)
        for cp in copies:
            cp.wait()
        return lax.dot_general(oh, stage_ref[0:rows, :], (((0,), (0,)), ((), ())),
                               preferred_element_type=F32)

    def finish(moe):
        x = x1_ref[...] + moe
        out_ref[...] = x * lax.rsqrt(jnp.mean(x * x, axis=1, keepdims=True) + RMS_EPS) * fg_ref[...]

    fast = _fits_fast(wins)

    @pl.when(fast)
    def _():
        oh = _one_hot(sel_ref, pos_ref, wins, all_experts, NP_FAST)
        finish(lax.dot_general(oh, ring_ref[slot], (((0,), (0,)), ((), ())), preferred_element_type=F32))

    @pl.when(jnp.logical_not(fast))
    def _():
        moe = jnp.zeros((TB, D_MODEL), F32)
        for g in range(N_EXPERTS // SLOW_GROUP):
            moe = moe + run_slow(list(range(g * SLOW_GROUP, (g + 1) * SLOW_GROUP)))
        finish(moe)


def _combine(starts, x1, sel, pos, fg, yg):
    t = x1.shape[0]
    nb = t // TB
    grid_spec = pltpu.PrefetchScalarGridSpec(
        num_scalar_prefetch=1,
        grid=(nb,),
        in_specs=[pl.BlockSpec((TB, D_MODEL), lambda b, s: (b, 0)),
                  pl.BlockSpec((N_EXPERTS, TB), lambda b, s: (0, b)),
                  pl.BlockSpec((N_EXPERTS, TB), lambda b, s: (0, b)),
                  pl.BlockSpec((1, D_MODEL), lambda b, s: (0, 0)),
                  pl.BlockSpec(memory_space=pl.ANY)],
        out_specs=pl.BlockSpec((TB, D_MODEL), lambda b, s: (b, 0)),
        scratch_shapes=[pltpu.VMEM((2, N_EXPERTS * NP_FAST, D_MODEL), BF16),
                        pltpu.VMEM((SLOW_GROUP * NP_SLOW, D_MODEL), BF16),
                        pltpu.SemaphoreType.DMA((2, N_EXPERTS)),
                        pltpu.SemaphoreType.DMA((N_EXPERTS,))],
    )
    return pl.pallas_call(
        functools.partial(_combine_kernel, nb=nb),
        grid_spec=grid_spec,
        out_shape=jax.ShapeDtypeStruct((t, D_MODEL), F32),
        compiler_params=_cparams(("arbitrary",)),
        name="combine",
    )(starts, x1, sel, pos, fg, yg)


def _rope_tables(seq):
    half = ROPE_DIM // 2
    inv_freq = jnp.power(ROPE_THETA, -jnp.arange(half, dtype=F32) * 2.0 / ROPE_DIM)
    ang = jnp.arange(seq).astype(F32)[:, None] * inv_freq[None, :]
    cos, sin = jnp.cos(ang), jnp.sin(ang)
    ones = jnp.ones((seq, ATT_QK_DIM - ROPE_DIM), F32)
    zeros = jnp.zeros((seq, ATT_QK_DIM - ROPE_DIM), F32)
    z8 = jnp.zeros((seq, half), F32)
    cos64 = jnp.concatenate([cos, cos, ones], axis=1)
    sa64 = jnp.concatenate([z8, sin, zeros], axis=1)
    sb64 = jnp.concatenate([-sin, z8, zeros], axis=1)
    tile = lambda a: jnp.concatenate([a, a], axis=1)
    return tile(cos64), tile(sa64), tile(sb64)


def _lane_row(v):
    return jnp.pad(v.astype(F32), ((0, 0), (0, LANES - v.shape[1])))[:, None, :]


def _trunk(x, p, lam_init):
    bsz, seq, _ = x.shape
    t = bsz * seq
    cap = CAPACITY_FACTOR * t // N_EXPERTS
    nb = t // TB
    assert seq % TM == 0 and seq % TC == 0 and seq % TQ == 0 and t % TB == 0
    assert cap % min(512, cap) == 0 and cap % ROW_ALIGN == 0 and SLOT_PAD % min(512, cap) == 0

    xf = x.reshape(t, D_MODEL)
    q, k, v, z, xbc, dt2 = _inproj(xf, p["norm1_g"], p["w_in"], *p["rope"](seq), seq)
    xs, bc = _conv(xbc.reshape(bsz, seq, XBC_COLS), p["conv_w"], p["conv_b"])
    y2 = _ssd(xs, bc, dt2.reshape(2, bsz, seq, LANES), p["dt_bias"], p["a_log"])
    att = _attn(q.reshape(bsz, seq, 512), k.reshape(bsz, seq, 512), v.reshape(bsz, seq, 512),
                p["lam_qk"], p["attn_sub_g"], lam_init)
    x1, h2aug, aff = _mixout(att.reshape(t, ATT_WIDTH), y2.reshape(2, t, SSM_WIDTH), xs.reshape(t, SSM_WIDTH),
                             z, xf, p["d_skip"], p["ssm_norm_g"], p["w_out"], p["norm2_g"], p["w_router"])
    sel, pos = _select(aff, cap)
    starts = jnp.concatenate([pos[:, ::TB].astype(jnp.int32),
                              jnp.full((N_EXPERTS, 1), cap, jnp.int32)], axis=1).reshape(-1)
    xe = _gather(starts, h2aug, sel, pos, cap)
    yg = _ffn(xe, p["w_gate"], p["w_up"], p["w_down"], cap)
    out = _combine(starts, x1, sel, pos, p["final_g"], yg)
    return out.reshape(bsz, seq, D_MODEL)


def kernel(x_prompt, x_sample, norm1_g, w_in, lam_qk, attn_sub_g, conv_w, conv_b, dt_bias, a_log, d_skip,
           ssm_norm_g, w_out, norm2_g, w_router, w_gate, w_up, w_down, final_g):
    depth = w_in.shape[0]
    assert depth == 1
    l = 0
    lam_init = 0.8 - 0.6 * math.exp(-0.3 * l)
    w = w_in[l]
    dt_cols = w[:, 3072:3088]
    pad = jnp.zeros((D_MODEL, LANES - SSM_HEADS), w.dtype)
    w_pad = jnp.concatenate([w[:, :3072], dt_cols[:, :SSM_HEADS], pad, dt_cols[:, SSM_HEADS:], pad], axis=1)
    tables = {}

    def rope(seq):
        if seq not in tables:
            tables[seq] = _rope_tables(seq)
        return tables[seq]

    p = {
        "norm1_g": norm1_g[l][None, :],
        "w_in": w_pad.astype(BF16),
        "rope": rope,
        "lam_qk": lam_qk[l],
        "attn_sub_g": attn_sub_g[l][None, :],
        "conv_w": conv_w[l],
        "conv_b": conv_b[l][None, :],
        "dt_bias": _lane_row(dt_bias[l]),
        "a_log": _lane_row(a_log[l]),
        "d_skip": jnp.repeat(d_skip[l], SSM_HEAD_DIM)[None, :],
        "ssm_norm_g": ssm_norm_g[l][None, :],
        "w_out": w_out[l].astype(BF16),
        "norm2_g": norm2_g[l][None, :],
        "w_router": jnp.pad(w_router[l], ((0, 0), (0, LANES - N_EXPERTS))).astype(BF16),
        "w_gate": w_gate[l].astype(BF16),
        "w_up": w_up[l].astype(BF16),
        "w_down": w_down[l].astype(BF16),
        "final_g": final_g[None, :],
    }
    return (_trunk(x_prompt, p, lam_init), _trunk(x_sample, p, lam_init))
```
